```python
import jax
import jax.numpy as jnp
from jax import lax
import numpy as np

D_MODEL = 1024
BATCH = 4
SEQ = 4096
DEPTH = 1
DEC_BATCH = 32
DEC_SEQ = 1
PAST_LEN = 16384
PAGE_SIZE = 128

D_RNN = D_MODEL
RNN_BLOCKS = 8
RNN_BLOCK = D_RNN // RNN_BLOCKS
CONV_W = 4
RGLRU_C = 8.0
N_HEADS = 8
HEAD_DIM = D_MODEL // N_HEADS
D_ATT = N_HEADS * HEAD_DIM
SB_BLOCK = 128
SB_BIAS_INIT = -7.0
D_IN = D_RNN + 3 * D_ATT + 2 * D_MODEL
PEER_HEADS = 8
PEER_TOPK = 16
N_KEYS = 128
N_EXPERTS = N_KEYS * N_KEYS
D_KEY = 256
D_KEY_HALF = D_KEY // 2
PEER_CHUNK = 128
LN_EPS = 1e-5
ALPHA = (2.0 * DEPTH) ** 0.25
BETA = (8.0 * DEPTH) ** -0.25

kernel_name = 'hybrid_rglru_stickbreak_peer_step'


def layer_norm(x, g, b):
    xf = x.astype(jnp.float32)
    mu = jnp.mean(xf, axis=-1, keepdims=True)
    var = jnp.mean(jnp.square(xf - mu), axis=-1, keepdims=True)
    y = (xf - mu) * lax.rsqrt(var + LN_EPS)
    return (y * g.astype(jnp.float32) + b.astype(jnp.float32)).astype(x.dtype)


def rglru_branch(xr, conv_buf, h0, conv_w, conv_b, w_a, b_a, w_x, b_x, lam):
    B, T, _ = xr.shape
    xp = jnp.concatenate([conv_buf.astype(xr.dtype), xr], axis=1)
    xc = conv_b + sum(conv_w[j] * xp[:, j:j + T] for j in range(CONV_W))
    new_buf = xp[:, T:]
    xb = xc.reshape(B, T, RNN_BLOCKS, RNN_BLOCK)
    r = jax.nn.sigmoid(jnp.einsum('btnd,nde->btne', xb, w_a).reshape(B, T, D_RNN) + b_a)
    i = jax.nn.sigmoid(jnp.einsum('btnd,nde->btne', xb, w_x).reshape(B, T, D_RNN) + b_x)
    log_a = -RGLRU_C * r.astype(jnp.float32) * jax.nn.softplus(-lam.astype(jnp.float32))
    a = jnp.exp(log_a)
    b = jnp.sqrt(-jnp.expm1(2.0 * log_a)) * (i * xc).astype(jnp.float32)
    b = b.at[:, 0].add(a[:, 0] * h0.astype(jnp.float32))

    def combine(lhs, rhs):
        a1, b1 = lhs
        a2, b2 = rhs
        return a1 * a2, a2 * b1 + b2

    _, h = lax.associative_scan(combine, (a, b), axis=1)
    return h.astype(xr.dtype), new_buf, h[:, -1].astype(xr.dtype)


def _sb_block(q, k, v, sb_bias, q_pos, k_pos):
    z = jnp.einsum('bqhd,bkhd->bhqk', q, k).astype(jnp.float32) * (HEAD_DIM ** -0.5)
    z = z + sb_bias.astype(jnp.float32)[None, :, None, None]
    causal = k_pos[None, :] < q_pos[:, None]
    log_keep = jnp.where(causal, jax.nn.log_sigmoid(-z), 0.0)
    suffix = lax.cumsum(log_keep, axis=3, reverse=True)
    log_between = jnp.concatenate([suffix[..., 1:], jnp.zeros_like(suffix[..., :1])], axis=-1)
    w = jnp.where(causal, jnp.exp(jax.nn.log_sigmoid(z) + log_between), 0.0)
    return jnp.einsum('bhqk,bkhd->bqhd', w.astype(v.dtype), v)


def stick_breaking(q, k, v, sb_bias, q_start):
    B, Tq, H, Dh = q.shape
    k_pos = jnp.arange(k.shape[1])
    if Tq <= SB_BLOCK:
        return _sb_block(q, k, v, sb_bias, q_start + jnp.arange(Tq), k_pos)
    nb = Tq // SB_BLOCK
    qb = q.reshape(B, nb, SB_BLOCK, H, Dh).transpose(1, 0, 2, 3, 4)

    def body(args):
        q_blk, blk = args
        return _sb_block(q_blk, k, v, sb_bias, q_start + blk * SB_BLOCK + jnp.arange(SB_BLOCK), k_pos)

    out = lax.map(body, (qb, jnp.arange(nb)))
    return out.transpose(1, 0, 2, 3, 4).reshape(B, Tq, H, Dh)


def peer_ffn(x, w_query, sub_keys, expert_u, expert_v):
    B, T, D = x.shape
    n = B * T
    c = min(PEER_CHUNK, n)
    pad = (-n) % c
    xp = jnp.pad(x.reshape(n, D), ((0, pad), (0, 0)))

    def chunk(xc):
        q = (xc @ w_query).reshape(c, PEER_HEADS, 2, D_KEY_HALF)
        s = jnp.einsum('chpd,hpkd->chpk', q, sub_keys).astype(jnp.float32)
        s1, i1 = lax.top_k(s[:, :, 0], PEER_TOPK)
        s2, i2 = lax.top_k(s[:, :, 1], PEER_TOPK)
        cand_s = (s1[..., :, None] + s2[..., None, :]).reshape(c, PEER_HEADS, PEER_TOPK * PEER_TOPK)
        cand_i = (i1[..., :, None] * N_KEYS + i2[..., None, :]).reshape(c, PEER_HEADS, PEER_TOPK * PEER_TOPK)
        top_s, pos = lax.top_k(cand_s, PEER_TOPK)
        idx = jnp.take_along_axis(cand_i, pos, axis=-1)
        g = jax.nn.softmax(top_s, axis=-1)
        u = expert_u[idx]
        act = jax.nn.gelu(jnp.einsum('chkd,cd->chk', u, xc))
        coef = (g * act.astype(jnp.float32)).astype(xc.dtype)
        return jnp.einsum('chk,chkd->cd', coef, expert_v[idx])

    out = lax.map(chunk, xp.reshape(-1, c, D))
    return out.reshape(-1, D)[:n].reshape(B, T, D)


def decoder_layer(x, conv_buf, h0, past_k, past_v, q_start,
                  w_in, b_gate, sb_bias, conv_w, conv_b, w_a, b_a, w_x, b_x, lam,
                  w_rnn_proj, w_att_proj, w_o, ln1_g, ln1_b,
                  w_query, sub_keys, expert_u, expert_v, ln2_g, ln2_b):
    B, T, _ = x.shape
    proj = jnp.einsum('btd,de->bte', x, w_in)
    o1 = D_RNN
    o2 = o1 + D_ATT
    o3 = o2 + D_ATT
    o4 = o3 + D_ATT
    xr = proj[..., :o1]
    q = proj[..., o1:o2].reshape(B, T, N_HEADS, HEAD_DIM)
    k = proj[..., o2:o3].reshape(B, T, N_HEADS, HEAD_DIM)
    v = proj[..., o3:o4].reshape(B, T, N_HEADS, HEAD_DIM)
    gates = jax.nn.sigmoid(proj[..., o4:] + b_gate)
    rnn_out, new_buf, h_last = rglru_branch(xr, conv_buf, h0, conv_w, conv_b, w_a, b_a, w_x, b_x, lam)
    keys = jnp.concatenate([past_k.astype(k.dtype), k], axis=1)
    vals = jnp.concatenate([past_v.astype(v.dtype), v], axis=1)
    att = stick_breaking(q, keys, vals, sb_bias, q_start).reshape(B, T, D_ATT)
    merged = gates[..., :D_MODEL] * (rnn_out @ w_rnn_proj) + gates[..., D_MODEL:] * (att @ w_att_proj)
    x = layer_norm(ALPHA * x + merged @ w_o, ln1_g, ln1_b)
    x = layer_norm(ALPHA * x + peer_ffn(x, w_query, sub_keys, expert_u, expert_v), ln2_g, ln2_b)
    return x, k, v, new_buf, h_last


def setup_inputs(seed: int = 0) -> dict:
    key = jax.random.key(seed)
    ks = jax.random.split(key, 32)
    n_pages = PAST_LEN // PAGE_SIZE
    n_phys = (5 * DEC_BATCH * n_pages) // 4
    f32 = jnp.float32
    nrm = lambda k, shape, s: jax.random.normal(k, shape, f32) * s
    x_prompt = nrm(ks[0], (BATCH, SEQ, D_MODEL), 1.0)
    x_sample = nrm(ks[1], (DEC_BATCH, DEC_SEQ, D_MODEL), 1.0)
    cache_k = nrm(ks[2], (DEPTH, n_phys, PAGE_SIZE, N_HEADS, HEAD_DIM), 1.0)
    cache_v = nrm(ks[3], (DEPTH, n_phys, PAGE_SIZE, N_HEADS, HEAD_DIM), 1.0)
    state_conv = nrm(ks[4], (DEPTH, DEC_BATCH, CONV_W - 1, D_RNN), 1.0)
    state_h = nrm(ks[5], (DEPTH, DEC_BATCH, D_RNN), 0.5)
    page_table = jax.random.permutation(ks[6], n_phys)[:DEC_BATCH * n_pages].reshape(DEC_BATCH, n_pages).astype(jnp.int32)
    w_in = nrm(ks[7], (DEPTH, D_MODEL, D_IN), D_MODEL ** -0.5)
    w_in = w_in.at[:, :, D_RNN + 2 * D_ATT:D_RNN + 3 * D_ATT].multiply(BETA)
    b_gate = nrm(ks[8], (DEPTH, 2 * D_MODEL), 0.1)
    sb_bias = SB_BIAS_INIT + nrm(ks[27], (DEPTH, N_HEADS), 0.1)
    conv_w = nrm(ks[9], (DEPTH, CONV_W, D_RNN), CONV_W ** -0.5)
    conv_b = nrm(ks[10], (DEPTH, D_RNN), 0.02)
    w_a = nrm(ks[11], (DEPTH, RNN_BLOCKS, RNN_BLOCK, RNN_BLOCK), RNN_BLOCK ** -0.5)
    b_a = nrm(ks[12], (DEPTH, D_RNN), 0.02)
    w_x = nrm(ks[13], (DEPTH, RNN_BLOCKS, RNN_BLOCK, RNN_BLOCK), RNN_BLOCK ** -0.5)
    b_x = nrm(ks[14], (DEPTH, D_RNN), 0.02)
    a_c = jax.random.uniform(ks[15], (DEPTH, D_RNN), f32, minval=0.9, maxval=0.999)
    base = a_c ** (1.0 / RGLRU_C)
    lam = jnp.log(base) - jnp.log1p(-base)
    w_rnn_proj = nrm(ks[16], (DEPTH, D_RNN, D_MODEL), BETA * D_RNN ** -0.5)
    w_att_proj = nrm(ks[17], (DEPTH, D_ATT, D_MODEL), BETA * D_ATT ** -0.5)
    w_o = nrm(ks[18], (DEPTH, D_MODEL, D_MODEL), BETA * D_MODEL ** -0.5)
    ln1_g = 1.0 + nrm(ks[19], (DEPTH, D_MODEL), 0.02)
    ln1_b = nrm(ks[20], (DEPTH, D_MODEL), 0.02)
    w_query = nrm(ks[21], (DEPTH, D_MODEL, PEER_HEADS * D_KEY), D_MODEL ** -0.5)
    sub_keys = nrm(ks[22], (DEPTH, PEER_HEADS, 2, N_KEYS, D_KEY_HALF), D_KEY_HALF ** -0.5)
    expert_u = nrm(ks[23], (DEPTH, N_EXPERTS, D_MODEL), D_MODEL ** -0.5)
    expert_v = nrm(ks[24], (DEPTH, N_EXPERTS, D_MODEL), BETA)
    ln2_g = 1.0 + nrm(ks[25], (DEPTH, D_MODEL), 0.02)
    ln2_b = nrm(ks[26], (DEPTH, D_MODEL), 0.02)
    return {'x_prompt': x_prompt, 'x_sample': x_sample, 'cache_k': cache_k, 'cache_v': cache_v,
            'state_conv': state_conv, 'state_h': state_h, 'page_table': page_table,
            'w_in': w_in, 'b_gate': b_gate, 'sb_bias': sb_bias, 'conv_w': conv_w, 'conv_b': conv_b,
            'w_a': w_a, 'b_a': b_a, 'w_x': w_x, 'b_x': b_x, 'lam': lam,
            'w_rnn_proj': w_rnn_proj, 'w_att_proj': w_att_proj, 'w_o': w_o,
            'ln1_g': ln1_g, 'ln1_b': ln1_b, 'w_query': w_query, 'sub_keys': sub_keys,
            'expert_u': expert_u, 'expert_v': expert_v, 'ln2_g': ln2_g, 'ln2_b': ln2_b}


def reference(x_prompt, x_sample, cache_k, cache_v, state_conv, state_h, page_table,
              w_in, b_gate, sb_bias, conv_w, conv_b, w_a, b_a, w_x, b_x, lam,
              w_rnn_proj, w_att_proj, w_o, ln1_g, ln1_b,
              w_query, sub_keys, expert_u, expert_v, ln2_g, ln2_b):
    n_b = x_prompt.shape[0]
    dec_b = x_sample.shape[0]
    past_len = page_table.shape[1] * PAGE_SIZE
    yp = x_prompt
    ys = x_sample
    kp, vp, cp, hp, ksm, vsm, csm, hsm = [], [], [], [], [], [], [], []
    for l in range(DEPTH):
        weights = (w_in[l], b_gate[l], sb_bias[l], conv_w[l], conv_b[l], w_a[l], b_a[l], w_x[l], b_x[l], lam[l],
                   w_rnn_proj[l], w_att_proj[l], w_o[l], ln1_g[l], ln1_b[l],
                   w_query[l], sub_keys[l], expert_u[l], expert_v[l], ln2_g[l], ln2_b[l])
        empty_kv = jnp.zeros((n_b, 0, N_HEADS, HEAD_DIM), x_prompt.dtype)
        zero_buf = jnp.zeros((n_b, CONV_W - 1, D_RNN), x_prompt.dtype)
        zero_h = jnp.zeros((n_b, D_RNN), x_prompt.dtype)
        yp, k_new, v_new, buf_new, h_new = decoder_layer(yp, zero_buf, zero_h, empty_kv, empty_kv, 0, *weights)
        kp.append(k_new)
        vp.append(v_new)
        cp.append(buf_new)
        hp.append(h_new)
        past_k = cache_k[l][page_table].reshape(dec_b, past_len, N_HEADS, HEAD_DIM)
        past_v = cache_v[l][page_table].reshape(dec_b, past_len, N_HEADS, HEAD_DIM)
        ys, k_new, v_new, buf_new, h_new = decoder_layer(ys, state_conv[l], state_h[l], past_k, past_v, past_len, *weights)
        ksm.append(k_new)
        vsm.append(v_new)
        csm.append(buf_new)
        hsm.append(h_new)
    k_prompt = jnp.stack(kp)
    v_prompt = jnp.stack(vp)
    conv_prompt = jnp.stack(cp)
    h_prompt = jnp.stack(hp)
    k_sample = jnp.stack(ksm)
    v_sample = jnp.stack(vsm)
    conv_sample = jnp.stack(csm)
    h_sample = jnp.stack(hsm)
    return (yp, ys, k_prompt, v_prompt, conv_prompt, h_prompt, k_sample, v_sample, conv_sample, h_sample)
```

```python
import functools

import jax
import jax.numpy as jnp
from jax import lax
from jax.experimental import pallas as pl
from jax.experimental.pallas import tpu as pltpu

F32 = jnp.float32
BF16 = jnp.bfloat16
I32 = jnp.int32

LANES = 128
SUBLANES = 8
BF16_ROWS = 16
VMEM_LIMIT_BYTES = 56 * 1024 * 1024

PEER_TOPK = 16
RGLRU_C = 8.0
LN_EPS = 1e-5

_NT = (((1,), (1,)), ((), ()))


def _cparams(*sem):
    return pltpu.CompilerParams(dimension_semantics=sem, vmem_limit_bytes=VMEM_LIMIT_BYTES)


def _tile(n, pref, mult=BF16_ROWS):
    best = None
    for t in range(mult, min(n, pref) + 1, mult):
        if n % t == 0:
            best = t
    return best if best is not None else n


def _resident(shape):
    nd = len(shape)
    return pl.BlockSpec(shape, lambda *_: (0,) * nd, pipeline_mode=pl.Buffered(1))


def _softplus(x):
    return jnp.maximum(x, 0.0) + jnp.log1p(jnp.exp(-jnp.abs(x)))


def _layer_norm(y, g, b):
    mu = jnp.mean(y, axis=-1, keepdims=True)
    yc = y - mu
    var = jnp.mean(yc * yc, axis=-1, keepdims=True)
    return yc * lax.rsqrt(var + LN_EPS) * g + b


def _in_proj_kernel(x_ref, w_ref, bg_ref, xr_ref, qb_ref, k_ref, kb_ref, v_ref, vb_ref,
                    gr_ref, ga_ref):
    d = x_ref.shape[1]
    x = x_ref[...].astype(BF16)

    def proj(j):
        return jnp.dot(x, w_ref[:, j * d:(j + 1) * d], preferred_element_type=F32)

    xr_ref[...] = proj(0)
    qb_ref[...] = proj(1).astype(BF16)
    k = proj(2)
    k_ref[...] = k
    kb_ref[...] = k.astype(BF16)
    v = proj(3)
    v_ref[...] = v
    vb_ref[...] = v.astype(BF16)
    gr_ref[...] = jax.nn.sigmoid(proj(4) + bg_ref[:, 0:d])
    ga_ref[...] = jax.nn.sigmoid(proj(5) + bg_ref[:, d:2 * d])


def _in_proj(x, w_in_bf, b_gate):
    n, d = x.shape
    tm = _tile(n, 256)
    row = pl.BlockSpec((tm, d), lambda i: (i, 0))
    f = jax.ShapeDtypeStruct((n, d), F32)
    h = jax.ShapeDtypeStruct((n, d), BF16)
    return pl.pallas_call(
        _in_proj_kernel,
        grid=(n // tm,),
        in_specs=[row, _resident(w_in_bf.shape), _resident(b_gate.shape)],
        out_specs=[row] * 8,
        out_shape=[f, h, f, h, f, h, f, f],
        compiler_params=_cparams("parallel"),
        name="in_proj",
    )(x, w_in_bf, b_gate)


def _rglru_gates(xc, wa_ref, ba, wx_ref, bx, lam):
    nb, bw = wa_ref.shape[0], wa_ref.shape[1]
    rs, gs = [], []
    for n in range(nb):
        xb = xc[:, n * bw:(n + 1) * bw].astype(BF16)
        rs.append(jnp.dot(xb, wa_ref[n], preferred_element_type=F32))
        gs.append(jnp.dot(xb, wx_ref[n], preferred_element_type=F32))
    r = jax.nn.sigmoid(jnp.concatenate(rs, axis=1) + ba)
    i = jax.nn.sigmoid(jnp.concatenate(gs, axis=1) + bx)
    log_a = (-RGLRU_C) * r * _softplus(-lam)
    a = jnp.exp(log_a)
    one_minus_a2 = -jnp.tanh(log_a) * (1.0 + a * a)
    b = jnp.sqrt(one_minus_a2) * (i * xc)
    return a, b


def _rglru_seq_kernel(xr_ref, buf_ref, h0_ref, cw_ref, cb_ref, wa_ref, ba_ref, wx_ref, bx_ref,
                      lam_ref, out_ref, hlast_ref, ext_ref, a_ref, b_ref, h_ref, carry_ref):
    t = pl.program_id(1)
    tc = xr_ref.shape[0]
    ncw = cw_ref.shape[0]
    pad = SUBLANES

    @pl.when(t == 0)
    def _():
        ext_ref[0:pad, :] = buf_ref[...]
        carry_ref[...] = h0_ref[...]

    ext_ref[pad:pad + tc, :] = xr_ref[...]
    xc = cb_ref[...] + cw_ref[0:1, :] * ext_ref[pad - ncw + 1:pad - ncw + 1 + tc, :]
    for j in range(1, ncw):
        off = pad - ncw + 1 + j
        xc = xc + cw_ref[j:j + 1, :] * ext_ref[off:off + tc, :]
    ext_ref[0:pad, :] = ext_ref[tc:tc + pad, :]

    a, b = _rglru_gates(xc, wa_ref, ba_ref[...], wx_ref, bx_ref[...], lam_ref[...])
    a_ref[...] = a
    b_ref[...] = b

    def step(s, h):
        h = a_ref[pl.ds(s, 1), :] * h + b_ref[pl.ds(s, 1), :]
        h_ref[pl.ds(s, 1), :] = h
        return h

    h = lax.fori_loop(0, tc, step, carry_ref[...], unroll=8)
    carry_ref[...] = h
    hlast_ref[...] = h
    out_ref[...] = h_ref[...].astype(BF16)


def _rglru_seq(xr, buf, h0, p):
    bsz, t, d = xr.shape
    ncw = p["conv_w"].shape[0]
    assert ncw - 1 <= SUBLANES and t >= SUBLANES
    tc = _tile(t, 256)
    hist = jnp.concatenate([jnp.zeros((bsz, SUBLANES - (ncw - 1), d), F32), buf], axis=1)
    vec = pl.BlockSpec((1, d), lambda b, s: (0, 0))
    out, hlast = pl.pallas_call(
        _rglru_seq_kernel,
        grid=(bsz, t // tc),
        in_specs=[
            pl.BlockSpec((None, tc, d), lambda b, s: (b, s, 0)),
            pl.BlockSpec((None, SUBLANES, d), lambda b, s: (b, 0, 0)),
            pl.BlockSpec((None, 1, d), lambda b, s: (b, 0, 0)),
            pl.BlockSpec((ncw, d), lambda b, s: (0, 0)),
            vec,
            pl.BlockSpec(p["w_a"].shape, lambda b, s: (0, 0, 0)),
            vec,
            pl.BlockSpec(p["w_x"].shape, lambda b, s: (0, 0, 0)),
            vec,
            vec,
        ],
        out_specs=[
            pl.BlockSpec((None, tc, d), lambda b, s: (b, s, 0)),
            pl.BlockSpec((None, 1, d), lambda b, s: (b, 0, 0)),
        ],
        out_shape=[jax.ShapeDtypeStruct((bsz, t, d), BF16),
                   jax.ShapeDtypeStruct((bsz, 1, d), F32)],
        scratch_shapes=[
            pltpu.VMEM((tc + SUBLANES, d), F32),
            pltpu.VMEM((tc, d), F32),
            pltpu.VMEM((tc, d), F32),
            pltpu.VMEM((tc, d), F32),
            pltpu.VMEM((1, d), F32),
        ],
        compiler_params=_cparams("parallel", "arbitrary"),
        name="rglru_seq",
    )(xr, hist, h0[:, None, :], p["conv_w"], p["conv_b"], p["w_a"], p["b_a"], p["w_x"],
      p["b_x"], p["lam"])
    return out, hlast[:, 0, :]


def _rglru_step_kernel(xr_ref, buf_ref, h0_ref, cw_ref, cb_ref, wa_ref, ba_ref, wx_ref, bx_ref,
                       lam_ref, out_ref, h_ref):
    ncw = cw_ref.shape[0]
    xc = cb_ref[...] + cw_ref[ncw - 1:ncw, :] * xr_ref[...]
    for j in range(ncw - 1):
        xc = xc + cw_ref[j:j + 1, :] * buf_ref[j]
    a, b = _rglru_gates(xc, wa_ref, ba_ref[...], wx_ref, bx_ref[...], lam_ref[...])
    h = a * h0_ref[...] + b
    h_ref[...] = h
    out_ref[...] = h.astype(BF16)


def _rglru_step(xr, buf, h0, p):
    bsz, d = xr.shape
    return pl.pallas_call(
        _rglru_step_kernel,
        out_shape=[jax.ShapeDtypeStruct((bsz, d), BF16), jax.ShapeDtypeStruct((bsz, d), F32)],
        compiler_params=pltpu.CompilerParams(vmem_limit_bytes=VMEM_LIMIT_BYTES),
        name="rglru_step",
    )(xr, jnp.swapaxes(buf, 0, 1), h0, p["conv_w"], p["conv_b"], p["w_a"], p["b_a"], p["w_x"],
      p["b_x"], p["lam"])


def _suffix_weights(group=1):
    r = lax.broadcasted_iota(I32, (LANES, LANES), 0)
    c = lax.broadcasted_iota(I32, (LANES, LANES), 1)
    same = (r % group) == (c % group)
    later = (r // group) > (c // group)
    return jnp.concatenate([(same & later).astype(BF16), same.astype(BF16)], axis=1)


def _sb_terms(z):
    sp = jnp.log1p(jnp.exp(-jnp.abs(z)))
    return jnp.minimum(z, 0.0) - sp, jnp.minimum(-z, 0.0) - sp


def _suffix_sums(lk, w_ref):
    hi = lk.astype(BF16)
    lo = (lk - hi.astype(F32)).astype(BF16)
    w = w_ref[...]
    cs = jnp.dot(hi, w, preferred_element_type=F32) + jnp.dot(lo, w, preferred_element_type=F32)
    return cs[:, :LANES], cs[:, LANES:]


def _sb_prompt_kernel(bias_ref, q_ref, k_ref, v_ref, w_ref, o_ref, acc_ref, carry_ref, *, scale):
    i = pl.program_id(1)
    tq, d = q_ref.shape
    nh = d // LANES
    q0 = i * tq
    acc_ref[...] = jnp.zeros_like(acc_ref)
    carry_ref[...] = jnp.zeros_like(carry_ref)

    def chunk(j, masked):
        k0 = pl.multiple_of(j * LANES, LANES)
        if masked:
            qpos = q0 + lax.broadcasted_iota(I32, (tq, LANES), 0)
            kpos = k0 + lax.broadcasted_iota(I32, (tq, LANES), 1)
            causal = kpos < qpos
        for h in range(nh):
            hs = slice(h * LANES, (h + 1) * LANES)
            s = lax.dot_general(q_ref[:, hs], k_ref[pl.ds(k0, LANES), hs], _NT,
                                preferred_element_type=F32)
            z = s * scale + bias_ref[h]
            ls, lk = _sb_terms(z)
            if masked:
                lk = jnp.where(causal, lk, 0.0)
            local, total = _suffix_sums(lk, w_ref)
            wgt = jnp.exp(ls + local + carry_ref[h])
            if masked:
                wgt = jnp.where(causal, wgt, 0.0)
            carry_ref[h] = carry_ref[h] + total
            acc_ref[h] = acc_ref[h] + jnp.dot(wgt.astype(BF16), v_ref[pl.ds(k0, LANES), hs],
                                              preferred_element_type=F32)

    nd = tq // LANES
    for dd in range(nd):
        chunk(i * nd + nd - 1 - dd, True)

    def body(jj, c):
        chunk(i * nd - 1 - jj, False)
        return c

    lax.fori_loop(0, i * nd, body, 0)
    for h in range(nh):
        o_ref[:, h * LANES:(h + 1) * LANES] = acc_ref[h].astype(BF16)


def _sb_prompt(qb, kb, vb, sb_bias, head_dim):
    bsz, t, d = qb.shape
    assert head_dim == LANES and t % LANES == 0
    tq = 256 if t % 256 == 0 else LANES
    nh = d // LANES
    kv = pl.BlockSpec((None, t, d), lambda b, i: (b, 0, 0), pipeline_mode=pl.Buffered(1))
    return pl.pallas_call(
        functools.partial(_sb_prompt_kernel, scale=head_dim ** -0.5),
        grid=(bsz, t // tq),
        in_specs=[
            pl.BlockSpec(memory_space=pltpu.SMEM),
            pl.BlockSpec((None, tq, d), lambda b, i: (b, i, 0)),
            kv, kv,
            _resident((LANES, 2 * LANES)),
        ],
        out_specs=pl.BlockSpec((None, tq, d), lambda b, i: (b, i, 0)),
        out_shape=jax.ShapeDtypeStruct((bsz, t, d), BF16),
        scratch_shapes=[pltpu.VMEM((nh, tq, LANES), F32), pltpu.VMEM((nh, tq, LANES), F32)],
        compiler_params=_cparams("parallel", "arbitrary"),
        name="sb_prompt",
    )(sb_bias, qb, kb, vb, _suffix_weights())


def _sb_decode_kernel(pt_ref, q_ref, bias_ref, own_ref, w_ref, *rest, scale, pps):
    k_refs, v_refs = rest[:pps], rest[pps:2 * pps]
    o_ref, acc_ref, carry_ref = rest[2 * pps:]
    s = pl.program_id(1)
    page, nh, dh = k_refs[0].shape
    flat = page * nh
    nchunk = flat // LANES

    @pl.when(s == 0)
    def _():
        acc_ref[...] = jnp.zeros_like(acc_ref)
        carry_ref[...] = jnp.zeros_like(carry_ref)

    q = q_ref[...]
    own = own_ref[...]
    for r in range(pps):
        kf = k_refs[r][...].reshape(flat, dh).astype(BF16)
        vf = v_refs[r][...].reshape(flat, dh).astype(BF16)
        zz = lax.dot_general(q, kf, _NT, preferred_element_type=F32) * own
        z = jnp.concatenate(
            [jnp.sum(zz[:, c * LANES:(c + 1) * LANES], axis=0, keepdims=True)
             for c in range(nchunk)], axis=0)
        z = z * scale + bias_ref[...]
        ls, lk = _sb_terms(z)
        local, total = _suffix_sums(lk, w_ref)
        run = carry_ref[...]
        later = [None] * nchunk
        for c in reversed(range(nchunk)):
            later[c] = run
            run = run + total[c:c + 1, :]
        carry_ref[...] = run
        wgt = jnp.exp(ls + local + jnp.concatenate(later, axis=0))
        wflat = jnp.concatenate([wgt[c:c + 1, :] for c in range(nchunk)], axis=1)
        wsel = (own * wflat).astype(BF16)
        acc_ref[...] = acc_ref[...] + jnp.dot(wsel, vf, preferred_element_type=F32)

    @pl.when(s == pl.num_programs(1) - 1)
    def _():
        o_ref[...] = acc_ref[...].astype(BF16)


def _sb_decode(qb, cache_k, cache_v, layer, page_table, sb_bias):
    bsz, d = qb.shape
    n_pages = page_table.shape[1]
    _, _, page, nh, dh = cache_k.shape
    flat = page * nh
    assert dh == LANES and LANES % nh == 0 and flat % LANES == 0
    rows = max(nh, BF16_ROWS)
    pps = 4 if n_pages % 4 == 0 else 1
    steps = n_pages // pps

    def page_spec(r):
        return pl.BlockSpec(
            (None, None, page, nh, dh),
            lambda b, s, pt: (layer, pt[b, n_pages - 1 - (s * pps + r)], 0, 0, 0))

    const = lambda shape: pl.BlockSpec(shape, lambda b, s, pt: (0,) * len(shape))
    grid_spec = pltpu.PrefetchScalarGridSpec(
        num_scalar_prefetch=1,
        grid=(bsz, steps),
        in_specs=[
            pl.BlockSpec((None, rows, dh), lambda b, s, pt: (b, 0, 0)),
            const((1, LANES)), const((rows, flat)), const((LANES, 2 * LANES)),
        ] + [page_spec(r) for r in range(pps)] * 2,
        out_specs=pl.BlockSpec((None, rows, dh), lambda b, s, pt: (b, 0, 0)),
        scratch_shapes=[pltpu.VMEM((rows, dh), F32), pltpu.VMEM((1, LANES), F32)],
    )
    q3 = jnp.pad(qb.reshape(bsz, nh, dh), ((0, 0), (0, rows - nh), (0, 0)))
    bias = jnp.tile(sb_bias, LANES // nh)[None, :]
    own = (jnp.arange(flat)[None, :] % nh == jnp.arange(rows)[:, None]).astype(F32)
    out = pl.pallas_call(
        functools.partial(_sb_decode_kernel, scale=dh ** -0.5, pps=pps),
        grid_spec=grid_spec,
        out_shape=jax.ShapeDtypeStruct((bsz, rows, dh), BF16),
        compiler_params=_cparams("parallel", "arbitrary"),
        name="sb_decode",
    )(page_table, q3, bias, own, _suffix_weights(nh), *([cache_k] * pps), *([cache_v] * pps))
    return out[:, :nh, :].reshape(bsz, d)


def _out_proj_kernel(x_ref, rnn_ref, att_ref, gr_ref, ga_ref, wr_ref, wa_ref, wo_ref, g_ref, b_ref,
                     y_ref, *, alpha):
    pr = jnp.dot(rnn_ref[...], wr_ref[...], preferred_element_type=F32)
    pa = jnp.dot(att_ref[...], wa_ref[...], preferred_element_type=F32)
    merged = gr_ref[...] * pr + ga_ref[...] * pa
    o = jnp.dot(merged.astype(BF16), wo_ref[...], preferred_element_type=F32)
    y_ref[...] = _layer_norm(alpha * x_ref[...] + o, g_ref[...], b_ref[...])


def _out_proj(x, rnn, att, g_rnn, g_att, p, alpha):
    n, d = x.shape
    tm = _tile(n, 512)
    row = pl.BlockSpec((tm, d), lambda i: (i, 0))
    vec = _resident((1, d))
    mat = _resident((d, d))
    return pl.pallas_call(
        functools.partial(_out_proj_kernel, alpha=alpha),
        grid=(n // tm,),
        in_specs=[row] * 5 + [mat] * 3 + [vec] * 2,
        out_specs=row,
        out_shape=jax.ShapeDtypeStruct((n, d), F32),
        compiler_params=_cparams("parallel"),
        name="out_proj_ln1",
    )(x, rnn, att, g_rnn, g_att, p["w_rnn_proj"], p["w_att_proj"], p["w_o"], p["ln1_g"],
      p["ln1_b"])


def _topk_rows(vals, k):
    m, n = vals.shape
    row = lax.broadcasted_iota(I32, (m, n), 0).astype(F32)
    slot = lax.broadcasted_iota(I32, (k, n), 0)
    out_v = jnp.zeros((k, n), F32)
    out_i = jnp.zeros((k, n), F32)
    for r in range(k):
        best = jnp.max(vals, axis=0, keepdims=True)
        idx = jnp.min(jnp.where(vals == best, row, float(m)), axis=0, keepdims=True)
        out_v = jnp.where(slot == r, best, out_v)
        out_i = jnp.where(slot == r, idx, out_i)
        vals = jnp.where(row == idx, -jnp.inf, vals)
    return out_v, out_i


def _select_rows(table, pos):
    out = jnp.zeros(pos.shape, table.dtype)
    for a in range(table.shape[0]):
        out = jnp.where(pos == float(a), table[a:a + 1, :], out)
    return out


def _peer_route_kernel(x_ref, wq_ref, sk_ref, ai_ref, bi_ref, g_ref):
    nheads, _, nkeys, dh = sk_ref.shape
    k = PEER_TOPK
    q = jnp.dot(x_ref[...].astype(BF16), wq_ref[...], preferred_element_type=F32).astype(BF16)
    a_rows, b_rows, g_rows = [], [], []
    for h in range(nheads):
        c0 = h * 2 * dh
        s1 = lax.dot_general(sk_ref[h, 0], q[:, c0:c0 + dh], _NT, preferred_element_type=F32)
        s2 = lax.dot_general(sk_ref[h, 1], q[:, c0 + dh:c0 + 2 * dh], _NT,
                             preferred_element_type=F32)
        v1, i1 = _topk_rows(s1, k)
        v2, i2 = _topk_rows(s2, k)
        cand = jnp.concatenate([v1[a:a + 1, :] + v2 for a in range(k)], axis=0)
        top, pos = _topk_rows(cand, k)
        pa = jnp.floor(pos * (1.0 / k))
        a_rows.append(_select_rows(i1, pa))
        b_rows.append(_select_rows(i2, pos - k * pa))
        ex = jnp.exp(top - jnp.max(top, axis=0, keepdims=True))
        g_rows.append(ex / jnp.sum(ex, axis=0, keepdims=True))
    ai_ref[...] = jnp.concatenate(a_rows, axis=0).T.astype(I32)
    bi_ref[...] = jnp.concatenate(b_rows, axis=0).T.astype(I32)
    g_ref[...] = jnp.concatenate(g_rows, axis=0).T


def _peer_route(x, wq_bf, sk_bf):
    n, d = x.shape
    npick = sk_bf.shape[0] * PEER_TOPK
    npad = -n % LANES
    xp = jnp.pad(x, ((0, npad), (0, 0))) if npad else x
    tm = _tile(n + npad, 256, LANES)
    row = pl.BlockSpec((tm, d), lambda i: (i, 0))
    out = pl.BlockSpec((tm, npick), lambda i: (i, 0))
    ai, bi, g = pl.pallas_call(
        _peer_route_kernel,
        grid=((n + npad) // tm,),
        in_specs=[row, _resident(wq_bf.shape), _resident(sk_bf.shape)],
        out_specs=[out] * 3,
        out_shape=[jax.ShapeDtypeStruct((n + npad, npick), I32),
                   jax.ShapeDtypeStruct((n + npad, npick), I32),
                   jax.ShapeDtypeStruct((n + npad, npick), F32)],
        compiler_params=_cparams("parallel"),
        name="peer_route",
    )(xp, wq_bf, sk_bf)
    return ai[:n], bi[:n], g[:n]


def _peer_scatter_kernel(ai_ref, bi_ref, g_ref, o_ref, gs_ref):
    tb, npick = ai_ref.shape
    nkeys = o_ref.shape[0]
    key = lax.broadcasted_iota(I32, (nkeys, npick), 0)

    def token(t, c):
        a_row = ai_ref[pl.ds(t, 1), :]
        b_row = bi_ref[pl.ds(t, 1), :]
        g_row = g_ref[pl.ds(t, 1), :]
        at = jnp.where(key == a_row, g_row, 0.0).astype(BF16)
        bt = jnp.where(key == b_row, 1.0, 0.0).astype(BF16)
        gt = lax.dot_general(at, bt, _NT, preferred_element_type=F32)
        gs_ref[pl.ds(t, nkeys, stride=tb), :] = gt
        return c

    lax.fori_loop(0, tb, token, 0, unroll=4)
    for a in range(nkeys):
        o_ref[a] = gs_ref[a * tb:(a + 1) * tb, :].astype(BF16)


def _peer_scatter(ai, bi, g, nkeys):
    n, npick = ai.shape
    tb = _tile(n, 128)
    row = pl.BlockSpec((tb, npick), lambda i: (i, 0))
    return pl.pallas_call(
        _peer_scatter_kernel,
        grid=(n // tb,),
        in_specs=[row] * 3,
        out_specs=pl.BlockSpec((nkeys, tb, nkeys), lambda i: (0, i, 0)),
        out_shape=jax.ShapeDtypeStruct((nkeys, n, nkeys), BF16),
        scratch_shapes=[pltpu.VMEM((nkeys * tb, nkeys), F32)],
        compiler_params=_cparams("parallel"),
        name="peer_scatter",
    )(ai, bi, g)


def _peer_mix_kernel(x_ref, gw_ref, u_ref, v_ref, g_ref, b_ref, y_ref, acc_ref, xb_ref, *, alpha):
    e = pl.program_id(1)
    na = gw_ref.shape[0]

    @pl.when(e == 0)
    def _():
        acc_ref[...] = jnp.zeros_like(acc_ref)
        xb_ref[...] = x_ref[...].astype(BF16)

    s = lax.dot_general(xb_ref[...], u_ref[...], _NT, preferred_element_type=F32)
    gw = jnp.concatenate([gw_ref[a] for a in range(na)], axis=1).astype(F32)
    coef = (jax.nn.gelu(s) * gw).astype(BF16)
    acc_ref[...] = acc_ref[...] + jnp.dot(coef, v_ref[...], preferred_element_type=F32)

    @pl.when(e == pl.num_programs(1) - 1)
    def _():
        y_ref[...] = _layer_norm(alpha * x_ref[...] + acc_ref[...], g_ref[...], b_ref[...])


def _peer_mix(x, gw, u_bf, v_bf, ln_g, ln_b, alpha):
    n, d = x.shape
    nkeys = gw.shape[0]
    tm = _tile(n, 1024)
    na = 4 if nkeys % 4 == 0 else 1
    te = na * nkeys
    row = pl.BlockSpec((tm, d), lambda i, e: (i, 0))
    vec = pl.BlockSpec((1, d), lambda i, e: (0, 0))
    return pl.pallas_call(
        functools.partial(_peer_mix_kernel, alpha=alpha),
        grid=(n // tm, nkeys // na),
        in_specs=[
            row,
            pl.BlockSpec((na, tm, nkeys), lambda i, e: (e, i, 0)),
            pl.BlockSpec((te, d), lambda i, e: (e, 0)),
            pl.BlockSpec((te, d), lambda i, e: (e, 0)),
            vec, vec,
        ],
        out_specs=row,
        out_shape=jax.ShapeDtypeStruct((n, d), F32),
        scratch_shapes=[pltpu.VMEM((tm, d), F32), pltpu.VMEM((tm, d), BF16)],
        compiler_params=_cparams("parallel", "arbitrary"),
        name="peer_mix_ln2",
    )(x, gw, u_bf, v_bf, ln_g, ln_b)


def _peer_and_ln2(x1, p, alpha):
    ai, bi, g = _peer_route(x1, p["w_query"], p["sub_keys"])
    gw = _peer_scatter(ai, bi, g, p["sub_keys"].shape[2])
    return _peer_mix(x1, gw, p["expert_u"], p["expert_v"], p["ln2_g"], p["ln2_b"], alpha)


def _layer_params(l, w_in, b_gate, sb_bias, conv_w, conv_b, w_a, b_a, w_x, b_x, lam, w_rnn_proj,
                  w_att_proj, w_o, ln1_g, ln1_b, w_query, sub_keys, expert_u, expert_v, ln2_g,
                  ln2_b):
    row = lambda a: a[l][None, :]
    return {
        "w_in": w_in[l].astype(BF16), "b_gate": row(b_gate), "sb_bias": sb_bias[l],
        "conv_w": conv_w[l], "conv_b": row(conv_b),
        "w_a": w_a[l].astype(BF16), "b_a": row(b_a), "w_x": w_x[l].astype(BF16), "b_x": row(b_x),
        "lam": row(lam),
        "w_rnn_proj": w_rnn_proj[l].astype(BF16), "w_att_proj": w_att_proj[l].astype(BF16),
        "w_o": w_o[l].astype(BF16), "ln1_g": row(ln1_g), "ln1_b": row(ln1_b),
        "w_query": w_query[l].astype(BF16), "sub_keys": sub_keys[l].astype(BF16),
        "expert_u": expert_u[l].astype(BF16), "expert_v": expert_v[l].astype(BF16),
        "ln2_g": row(ln2_g), "ln2_b": row(ln2_b),
    }


def kernel(x_prompt, x_sample, cache_k, cache_v, state_conv, state_h, page_table, w_in, b_gate,
           sb_bias, conv_w, conv_b, w_a, b_a, w_x, b_x, lam, w_rnn_proj, w_att_proj, w_o, ln1_g,
           ln1_b, w_query, sub_keys, expert_u, expert_v, ln2_g, ln2_b):
    depth = w_in.shape[0]
    nb, t, d = x_prompt.shape
    db, dt, _ = x_sample.shape
    n_heads, head_dim = cache_k.shape[3], cache_k.shape[4]
    ncw = conv_w.shape[1]
    assert dt == 1, "the sample group advances one position per step"
    assert w_in.shape[2] == 6 * d and n_heads * head_dim == d and w_a.shape[2] == LANES
    alpha = (2.0 * depth) ** 0.25

    yp = x_prompt.reshape(nb * t, d)
    ys = x_sample.reshape(db, d)
    outs = [[] for _ in range(8)]
    for l in range(depth):
        p = _layer_params(l, w_in, b_gate, sb_bias, conv_w, conv_b, w_a, b_a, w_x, b_x, lam,
                          w_rnn_proj, w_att_proj, w_o, ln1_g, ln1_b, w_query, sub_keys, expert_u,
                          expert_v, ln2_g, ln2_b)

        xr, qb, k, kb, v, vb, g_rnn, g_att = _in_proj(yp, p["w_in"], p["b_gate"])
        xr3 = xr.reshape(nb, t, d)
        rnn, h_last = _rglru_seq(xr3, jnp.zeros((nb, ncw - 1, d), F32), jnp.zeros((nb, d), F32), p)
        att = _sb_prompt(qb.reshape(nb, t, d), kb.reshape(nb, t, d), vb.reshape(nb, t, d),
                         p["sb_bias"], head_dim)
        x1 = _out_proj(yp, rnn.reshape(nb * t, d), att.reshape(nb * t, d), g_rnn, g_att, p, alpha)
        yp = _peer_and_ln2(x1, p, alpha)
        outs[0].append(k.reshape(nb, t, n_heads, head_dim))
        outs[1].append(v.reshape(nb, t, n_heads, head_dim))
        outs[2].append(xr3[:, t - (ncw - 1):, :])
        outs[3].append(h_last)

        xr, qb, k, kb, v, vb, g_rnn, g_att = _in_proj(ys, p["w_in"], p["b_gate"])
        rnn, h_new = _rglru_step(xr, state_conv[l], state_h[l], p)
        att = _sb_decode(qb, cache_k, cache_v, l, page_table, p["sb_bias"])
        x1 = _out_proj(ys, rnn, att, g_rnn, g_att, p, alpha)
        ys = _peer_and_ln2(x1, p, alpha)
        outs[4].append(k.reshape(db, dt, n_heads, head_dim))
        outs[5].append(v.reshape(db, dt, n_heads, head_dim))
        outs[6].append(jnp.concatenate([state_conv[l][:, 1:, :], xr[:, None, :]], axis=1))
        outs[7].append(h_new)

    stacked = [jnp.stack(o) for o in outs]
    return (yp.reshape(nb, t, d), ys.reshape(db, dt, d), *stacked)
```
